```python
import jax, jax.numpy as jnp
from jax import lax
import numpy as np

D_MODEL = 2048
BATCH = 8
SEQ = 2048
DEPTH = 1
DEC_BATCH = 4
DEC_SEQ = 2048
PAST_LEN = 128

HEAD_DIM = 128
N_MIX_HEADS = D_MODEL // HEAD_DIM
A_HEADS = N_MIX_HEADS // 2
A_KV_HEADS = A_HEADS // 4
B_HEADS = N_MIX_HEADS // 4
M_HEADS = N_MIX_HEADS // 4
A_Q = A_HEADS * HEAD_DIM
A_KV = A_KV_HEADS * HEAD_DIM
B_W = B_HEADS * HEAD_DIM
M_W = M_HEADS * HEAD_DIM
IN_WIDTH = A_Q + 2 * A_KV + 3 * B_W + M_W
MIX_WIDTH = A_Q + B_W + M_W
N_MEM = 256
GRID_W = 64
AXIAL_THETA = 10000.0
ROPE_THETA = 500000.0
ROPE_DIMS = HEAD_DIM // 4
DILATED_PATTERNS = ((128, 1), (512, 4), (2048, 16))
Q_BLOCK = 128
N_GROUPS = 4
EXPERTS_PER_GROUP = 8
N_EXPERTS = N_GROUPS * EXPERTS_PER_GROUP
TOP_K = 2
D_EXPERT = D_MODEL // 2
MOE_BLOCK = 128
EPS = 1e-6

kernel_name = 'hymba_axial_dilated_hiermoe_encoder'


def rms_norm(x, g):
    xf = x.astype(jnp.float32)
    y = xf * lax.rsqrt(jnp.mean(xf * xf, axis=-1, keepdims=True) + EPS)
    return (y * g.astype(jnp.float32)).astype(x.dtype)


def rope_cos_sin(pos, dims, theta):
    inv = jnp.exp(-np.log(theta) * jnp.arange(0, dims, 2, dtype=jnp.float32) / dims)
    ang = pos.astype(jnp.float32)[:, None] * inv[None, :]
    return jnp.cos(ang), jnp.sin(ang)


def apply_rope(x, cos, sin):
    half = x.shape[-1] // 2
    x1, x2 = x[..., :half], x[..., half:]
    c = cos[None, :, None, :].astype(x.dtype)
    s = sin[None, :, None, :].astype(x.dtype)
    return jnp.concatenate([x1 * c - x2 * s, x2 * c + x1 * s], axis=-1)


def axial_gqa(q, k, v, q_g, k_g):
    B, S = q.shape[0], q.shape[1]
    rows = S // GRID_W
    row = jnp.repeat(jnp.arange(rows), GRID_W)
    col = jnp.tile(jnp.arange(GRID_W), rows)
    half = HEAD_DIM // 2
    cr, sr = rope_cos_sin(row, half, AXIAL_THETA)
    cc, sc = rope_cos_sin(col, half, AXIAL_THETA)

    def axial(t):
        return jnp.concatenate([apply_rope(t[..., :half], cr, sr),
                                apply_rope(t[..., half:], cc, sc)], axis=-1)

    q = axial(rms_norm(q, q_g))
    k = axial(rms_norm(k, k_g))
    rep = A_HEADS // A_KV_HEADS
    nb = S // Q_BLOCK
    qb = q.reshape(B, nb, Q_BLOCK, A_KV_HEADS, rep, HEAD_DIM).transpose(1, 0, 2, 3, 4, 5)
    scale = HEAD_DIM ** -0.5

    def block(qi):
        s = jnp.einsum('bqgrd,bsgd->bgrqs', qi, k, preferred_element_type=jnp.float32) * scale
        p = jax.nn.softmax(s, axis=-1).astype(v.dtype)
        return jnp.einsum('bgrqs,bsgd->bqgrd', p, v)

    o = lax.map(block, qb)
    return o.transpose(1, 0, 2, 3, 4, 5).reshape(B, S, A_Q)


def banded_window_attn(q, k, v, half):
    B, L, H, D = q.shape
    blk = half
    nb = -(-L // blk)
    Lp = nb * blk
    qp = jnp.pad(q, ((0, 0), (0, Lp - L), (0, 0), (0, 0)))
    kv_pad = ((0, 0), (blk, Lp - L + blk), (0, 0), (0, 0))
    kp = jnp.pad(k, kv_pad)
    vp = jnp.pad(v, kv_pad)
    kidx = jnp.arange(nb)[:, None] * blk + jnp.arange(3 * blk)[None, :]
    kpos = kidx - blk
    qpos = jnp.arange(Lp).reshape(nb, blk)
    kw = kp[:, kidx]
    vw = vp[:, kidx]
    mask = ((jnp.abs(qpos[:, :, None] - kpos[:, None, :]) <= half)
            & (kpos[:, None, :] >= 0) & (kpos[:, None, :] < L))
    qb = qp.reshape(B, nb, blk, H, D)
    s = jnp.einsum('bnqhd,bnkhd->bnhqk', qb, kw, preferred_element_type=jnp.float32) * (D ** -0.5)
    s = jnp.where(mask[None, :, None], s, -jnp.inf)
    m = jnp.max(s, axis=-1, keepdims=True)
    p = jnp.exp(s - m)
    den = jnp.sum(p, axis=-1, keepdims=True)
    o = jnp.einsum('bnhqk,bnkhd->bnqhd', (p / den).astype(v.dtype), vw)
    lse = (m + jnp.log(den))[..., 0]
    o = o.reshape(B, Lp, H, D)[:, :L]
    lse = lse.transpose(0, 1, 3, 2).reshape(B, Lp, H)[:, :L]
    return o, lse


def dilated_mixture(q, k, v):
    B, S, H, D = q.shape
    c, s = rope_cos_sin(jnp.arange(S), ROPE_DIMS, ROPE_THETA)

    def rot(t):
        return jnp.concatenate([apply_rope(t[..., :ROPE_DIMS], c, s), t[..., ROPE_DIMS:]], axis=-1)

    q, k = rot(q), rot(k)
    outs, lses = [], []
    for window, dil in DILATED_PATTERNS:
        L = S // dil
        fq, fk, fv = (t.reshape(B, L, dil * H, D) for t in (q, k, v))
        o, lse = banded_window_attn(fq, fk, fv, window // (2 * dil))
        outs.append(o.reshape(B, S, H, D))
        lses.append(lse.reshape(B, S, H))
    w = jax.nn.softmax(jnp.stack(lses), axis=0)
    o = jnp.einsum('pbsh,pbshd->bshd', w.astype(v.dtype), jnp.stack(outs))
    return o.reshape(B, S, B_W)


def memory_attn(q, mem, mem_g, w_mem_kv):
    B, S = q.shape[0], q.shape[1]
    n = mem.shape[1]
    kv = (rms_norm(mem, mem_g) @ w_mem_kv).reshape(B, n, 2, M_HEADS, HEAD_DIM)
    s = jnp.einsum('bqhd,bkhd->bhqk', q, kv[:, :, 0], preferred_element_type=jnp.float32) * (HEAD_DIM ** -0.5)
    p = jax.nn.softmax(s, axis=-1).astype(q.dtype)
    o = jnp.einsum('bhqk,bkhd->bqhd', p, kv[:, :, 1])
    return o.reshape(B, S, M_W)


def hier_moe(h, w_rg, b_rg, w_re, b_re, w_gate, w_up, w_down):
    T, D = h.shape
    g_logits = (h @ w_rg).astype(jnp.float32) + b_rg.astype(jnp.float32)
    g_prob = jax.nn.softmax(g_logits, axis=-1)
    g_idx = jnp.argmax(g_logits, axis=-1).astype(jnp.int32)
    g_w = jnp.take_along_axis(g_prob, g_idx[:, None], axis=-1)
    e_logits = ((h @ w_re).astype(jnp.float32) + b_re.astype(jnp.float32)).reshape(T, N_GROUPS, EXPERTS_PER_GROUP)
    e_logits = jnp.take_along_axis(e_logits, g_idx[:, None, None], axis=1)[:, 0]
    top_v, top_i = lax.top_k(e_logits, TOP_K)
    top_w = jax.nn.softmax(top_v, axis=-1) * g_w
    expert_id = g_idx[:, None] * EXPERTS_PER_GROUP + top_i.astype(jnp.int32)

    A = T * TOP_K
    flat_e = expert_id.reshape(A)
    flat_w = top_w.reshape(A)
    flat_tok = jnp.arange(A, dtype=jnp.int32) // TOP_K
    order = jnp.argsort(flat_e)
    se = flat_e[order]
    counts = jnp.zeros((N_EXPERTS,), jnp.int32).at[flat_e].add(1)
    pcounts = (counts + MOE_BLOCK - 1) // MOE_BLOCK * MOE_BLOCK
    pend = jnp.cumsum(pcounts)
    pstart = pend - pcounts
    start = jnp.cumsum(counts) - counts
    dest = pstart[se] + jnp.arange(A, dtype=jnp.int32) - start[se]
    nblk = -(-A // MOE_BLOCK) + N_EXPERTS
    cap = nblk * MOE_BLOCK
    buf_tok = jnp.zeros((cap,), jnp.int32).at[dest].set(flat_tok[order])
    buf_w = jnp.zeros((cap,), h.dtype).at[dest].set(flat_w[order].astype(h.dtype))
    blk_e = jnp.minimum(jnp.searchsorted(pend, jnp.arange(nblk, dtype=jnp.int32) * MOE_BLOCK, side='right'),
                        N_EXPERTS - 1).astype(jnp.int32)
    xb = h[buf_tok].reshape(nblk, MOE_BLOCK, D)

    def expert_block(args):
        xe, e = args
        return (jax.nn.silu(xe @ w_gate[e]) * (xe @ w_up[e])) @ w_down[e]

    yb = lax.map(expert_block, (xb, blk_e)).reshape(cap, D)
    return jnp.zeros_like(h).at[buf_tok].add(yb * buf_w[:, None])


def encoder(x, mem, attn_norm_g, w_in, q_norm_g, k_norm_g, mem_norm_g, w_mem_kv, w_out,
            ffn_norm_g, w_router_group, b_router_group, w_router_expert, b_router_expert,
            w_gate, w_up, w_down, final_norm_g):
    B, S, D = x.shape
    offs = np.cumsum([A_Q, A_KV, A_KV, B_W, B_W, B_W])
    for l in range(DEPTH):
        h = rms_norm(x, attn_norm_g[l])
        proj = h @ w_in[l]
        aq, ak, av, bq, bk, bv, mq = jnp.split(proj, list(offs), axis=-1)
        oa = axial_gqa(aq.reshape(B, S, A_HEADS, HEAD_DIM), ak.reshape(B, S, A_KV_HEADS, HEAD_DIM),
                       av.reshape(B, S, A_KV_HEADS, HEAD_DIM), q_norm_g[l], k_norm_g[l])
        ob = dilated_mixture(bq.reshape(B, S, B_HEADS, HEAD_DIM), bk.reshape(B, S, B_HEADS, HEAD_DIM),
                             bv.reshape(B, S, B_HEADS, HEAD_DIM))
        om = memory_attn(mq.reshape(B, S, M_HEADS, HEAD_DIM), mem, mem_norm_g[l], w_mem_kv[l])
        x = x + jnp.concatenate([oa, ob, om], axis=-1) @ w_out[l]
        h = rms_norm(x, ffn_norm_g[l]).reshape(B * S, D)
        x = x + hier_moe(h, w_router_group[l], b_router_group[l], w_router_expert[l], b_router_expert[l],
                         w_gate[l], w_up[l], w_down[l]).reshape(B, S, D)
    return rms_norm(x, final_norm_g)


def setup_inputs(seed: int = 0) -> dict:
    key = jax.random.key(seed)
    ks = jax.random.split(key, 24)
    f32 = jnp.float32

    def nrm(k, shape, scale):
        return jax.random.normal(k, shape, f32) * scale

    def gain(k, shape):
        return 1.0 + 0.02 * jax.random.normal(k, shape, f32)

    return {
        'x_prompt': nrm(ks[0], (BATCH, SEQ, D_MODEL), 1.0),
        'x_sample': nrm(ks[1], (DEC_BATCH, DEC_SEQ, D_MODEL), 1.0),
        'mem_prompt': nrm(ks[2], (BATCH, N_MEM, D_MODEL), 1.0),
        'mem_sample': nrm(ks[3], (DEC_BATCH, N_MEM, D_MODEL), 1.0),
        'attn_norm_g': gain(ks[4], (DEPTH, D_MODEL)),
        'w_in': nrm(ks[5], (DEPTH, D_MODEL, IN_WIDTH), D_MODEL ** -0.5),
        'q_norm_g': gain(ks[6], (DEPTH, HEAD_DIM)),
        'k_norm_g': gain(ks[7], (DEPTH, HEAD_DIM)),
        'mem_norm_g': gain(ks[8], (DEPTH, D_MODEL)),
        'w_mem_kv': nrm(ks[9], (DEPTH, D_MODEL, 2 * M_W), D_MODEL ** -0.5),
        'w_out': nrm(ks[10], (DEPTH, MIX_WIDTH, D_MODEL), MIX_WIDTH ** -0.5),
        'ffn_norm_g': gain(ks[11], (DEPTH, D_MODEL)),
        'w_router_group': nrm(ks[12], (DEPTH, D_MODEL, N_GROUPS), D_MODEL ** -0.5),
        'b_router_group': nrm(ks[13], (DEPTH, N_GROUPS), 0.01),
        'w_router_expert': nrm(ks[14], (DEPTH, D_MODEL, N_EXPERTS), D_MODEL ** -0.5),
        'b_router_expert': nrm(ks[15], (DEPTH, N_EXPERTS), 0.01),
        'w_gate': nrm(ks[16], (DEPTH, N_EXPERTS, D_MODEL, D_EXPERT), D_MODEL ** -0.5),
        'w_up': nrm(ks[17], (DEPTH, N_EXPERTS, D_MODEL, D_EXPERT), D_MODEL ** -0.5),
        'w_down': nrm(ks[18], (DEPTH, N_EXPERTS, D_EXPERT, D_MODEL), D_EXPERT ** -0.5),
        'final_norm_g': gain(ks[19], (D_MODEL,)),
    }


def reference(x_prompt, x_sample, mem_prompt, mem_sample, attn_norm_g, w_in, q_norm_g, k_norm_g,
              mem_norm_g, w_mem_kv, w_out, ffn_norm_g, w_router_group, b_router_group,
              w_router_expert, b_router_expert, w_gate, w_up, w_down, final_norm_g):
    y_prompt = encoder(x_prompt, mem_prompt, attn_norm_g, w_in, q_norm_g, k_norm_g, mem_norm_g,
                       w_mem_kv, w_out, ffn_norm_g, w_router_group, b_router_group, w_router_expert,
                       b_router_expert, w_gate, w_up, w_down, final_norm_g)
    y_sample = encoder(x_sample, mem_sample, attn_norm_g, w_in, q_norm_g, k_norm_g, mem_norm_g,
                       w_mem_kv, w_out, ffn_norm_g, w_router_group, b_router_group, w_router_expert,
                       b_router_expert, w_gate, w_up, w_down, final_norm_g)
    return (y_prompt, y_sample)
```

```python
import functools

import numpy as np
import jax
import jax.numpy as jnp
from jax import lax
from jax.experimental import pallas as pl
from jax.experimental.pallas import tpu as pltpu

F32 = jnp.float32
BF16 = jnp.bfloat16

HEAD_DIM = 128
LANES = 128
GRID_W = 64
AXIAL_THETA = 10000.0
ROPE_THETA = 500000.0
ROPE_DIMS = HEAD_DIM // 4
DILATED_PATTERNS = ((128, 1), (512, 4), (2048, 16))
N_GROUPS = 4
EXPERTS_PER_GROUP = 8
N_EXPERTS = N_GROUPS * EXPERTS_PER_GROUP
TOP_K = 2
EPS = 1e-6
EXPERT_LANE0 = 32
VMEM_LIMIT = 56 * 1024 * 1024

ROW_TILE = 256
Q_TILE = 256
MOE_TILE = 256


def _cparams(sem):
    return pltpu.CompilerParams(dimension_semantics=sem, vmem_limit_bytes=VMEM_LIMIT)


def _rms(x, g):
    ms = jnp.mean(x * x, axis=-1, keepdims=True)
    return x * lax.rsqrt(ms + EPS) * g


def _rope(t, c, s, first_half, shift):
    partner = jnp.where(first_half, pltpu.roll(t, LANES - shift, 1), pltpu.roll(t, shift, 1))
    return t * c + partner * s


def _inproj_kernel(x_ref, g_ref, w_ref, qg_ref, kg_ref, ca_ref, sa_ref, cd_ref, sd_ref, o_ref,
                   *, n_aq, n_akv, n_b, n_m):
    tm = x_ref.shape[0]
    h = _rms(x_ref[...], g_ref[...]).astype(BF16)
    lane = lax.broadcasted_iota(jnp.int32, (tm, LANES), 1)
    ax_first = (lane & 63) < 32
    dl_first = (lane & 31) < 16
    scale = HEAD_DIM ** -0.5
    kinds = (["aq"] * n_aq + ["ak"] * n_akv + ["v"] * n_akv + ["bq"] * n_b + ["bk"] * n_b
             + ["v"] * n_b + ["mq"] * n_m)
    chunk = 4
    for c0 in range(0, len(kinds), chunk):
        p = jnp.dot(h, w_ref[:, c0 * LANES:(c0 + chunk) * LANES], preferred_element_type=F32)
        for j in range(chunk):
            t = p[:, j * LANES:(j + 1) * LANES]
            kind = kinds[c0 + j]
            if kind == "aq":
                t = _rope(_rms(t, qg_ref[...]), ca_ref[...], sa_ref[...], ax_first, 32) * scale
            elif kind == "ak":
                t = _rope(_rms(t, kg_ref[...]), ca_ref[...], sa_ref[...], ax_first, 32)
            elif kind == "bq":
                t = _rope(t, cd_ref[...], sd_ref[...], dl_first, 16) * scale
            elif kind == "bk":
                t = _rope(t, cd_ref[...], sd_ref[...], dl_first, 16)
            elif kind == "mq":
                t = t * scale
            o_ref[:, (c0 + j) * LANES:(c0 + j + 1) * LANES] = t.astype(BF16)


def _inproj(x, g, w, qg, kg, tabs, seq, head_counts):
    T, D = x.shape
    N = w.shape[1]
    tm = ROW_TILE
    nseq = seq // tm
    row = lambda i: (i, 0)
    fixed = lambda i: (0, 0)
    tab = lambda i: (i % nseq, 0)
    n_aq, n_akv, n_b, n_m = head_counts
    return pl.pallas_call(
        functools.partial(_inproj_kernel, n_aq=n_aq, n_akv=n_akv, n_b=n_b, n_m=n_m),
        grid=(T // tm,),
        in_specs=[pl.BlockSpec((tm, D), row), pl.BlockSpec((1, D), fixed),
                  pl.BlockSpec((D, N), fixed),
                  pl.BlockSpec((1, LANES), fixed), pl.BlockSpec((1, LANES), fixed),
                  pl.BlockSpec((tm, LANES), tab), pl.BlockSpec((tm, LANES), tab),
                  pl.BlockSpec((tm, LANES), tab), pl.BlockSpec((tm, LANES), tab)],
        out_specs=pl.BlockSpec((tm, N), row),
        out_shape=jax.ShapeDtypeStruct((T, N), BF16),
        compiler_params=_cparams(("arbitrary",)),
        name="inproj",
    )(x, g, w, qg, kg, *tabs)


def _norm_matmul_kernel(x_ref, g_ref, w_ref, o_ref):
    h = _rms(x_ref[...], g_ref[...]).astype(BF16)
    o_ref[...] = jnp.dot(h, w_ref[...], preferred_element_type=F32).astype(o_ref.dtype)


def _norm_matmul(x, g, w):
    T, D = x.shape
    N = w.shape[1]
    tm = ROW_TILE
    return pl.pallas_call(
        _norm_matmul_kernel,
        grid=(T // tm,),
        in_specs=[pl.BlockSpec((tm, D), lambda i: (i, 0)), pl.BlockSpec((1, D), lambda i: (0, 0)),
                  pl.BlockSpec((D, N), lambda i: (0, 0))],
        out_specs=pl.BlockSpec((tm, N), lambda i: (i, 0)),
        out_shape=jax.ShapeDtypeStruct((T, N), BF16),
        compiler_params=_cparams(("arbitrary",)),
        name="mem_kv",
    )(x, g, w)


def _attn_kernel(*refs, rep, has_bias):
    if has_bias:
        q_ref, k_ref, v_ref, b_ref, o_ref = refs
    else:
        q_ref, k_ref, v_ref, o_ref = refs
    k = k_ref[...]
    v = v_ref[...]
    for r in range(rep):
        q = q_ref[:, r * LANES:(r + 1) * LANES]
        s = lax.dot_general(q, k, (((1,), (1,)), ((), ())), preferred_element_type=F32)
        if has_bias:
            s = s + b_ref[...]
        m = jnp.max(s, axis=-1, keepdims=True)
        p = jnp.exp(s - m)
        l = jnp.sum(p, axis=-1, keepdims=True)
        o = jnp.dot(p.astype(BF16), v, preferred_element_type=F32)
        o_ref[:, r * LANES:(r + 1) * LANES] = (o / l).astype(o_ref.dtype)


def _attention(q_arr, k_arr, v_arr, bias, *, grid, q_map, k_map, v_map, b_map, o_map,
               rep, sk, out_cols, name):
    T = q_arr.shape[0]
    tq = Q_TILE
    in_specs = [pl.BlockSpec((tq, rep * LANES), q_map), pl.BlockSpec((sk, LANES), k_map),
                pl.BlockSpec((sk, LANES), v_map)]
    args = [q_arr, k_arr, v_arr]
    if bias is not None:
        in_specs.append(pl.BlockSpec((tq, sk), b_map))
        args.append(bias)
    return pl.pallas_call(
        functools.partial(_attn_kernel, rep=rep, has_bias=bias is not None),
        grid=grid,
        in_specs=in_specs,
        out_specs=pl.BlockSpec((tq, rep * LANES), o_map),
        out_shape=jax.ShapeDtypeStruct((T, out_cols), BF16),
        compiler_params=_cparams(("arbitrary",) * len(grid)),
        name=name,
    )(*args)


def _outproj_kernel(x_ref, oa_ref, ob_ref, om_ref, wo_ref, g_ref, wrc_ref, wrh_ref, br_ref,
                    x1_ref, h_ref, route_ref, *, na, nb):
    tm = x_ref.shape[0]
    acc = x_ref[...]
    acc = acc + jnp.dot(oa_ref[...], wo_ref[0:na, :], preferred_element_type=F32)
    acc = acc + jnp.dot(ob_ref[...], wo_ref[na:na + nb, :], preferred_element_type=F32)
    acc = acc + jnp.dot(om_ref[...], wo_ref[na + nb:, :], preferred_element_type=F32)
    x1_ref[...] = acc
    h = _rms(acc, g_ref[...])
    h_ref[...] = h.astype(h_ref.dtype)

    hh = h.astype(BF16)
    hl = (h - hh.astype(F32)).astype(BF16)
    a = jnp.dot(hh, wrc_ref[...], preferred_element_type=F32)
    lg = (a[:, :LANES] + a[:, LANES:] + jnp.dot(hl, wrh_ref[...], preferred_element_type=F32)
          + br_ref[...])

    lane = lax.broadcasted_iota(jnp.int32, (tm, LANES), 1)
    lanef = lane.astype(F32)
    neg = jnp.float32(-jnp.inf)
    big = jnp.float32(LANES)
    gl = jnp.where(lane < N_GROUPS, lg, neg)
    gmax = jnp.max(gl, axis=-1, keepdims=True)
    gidx = jnp.min(jnp.where(gl == gmax, lanef, big), axis=-1, keepdims=True)
    gw = 1.0 / jnp.sum(jnp.exp(gl - gmax), axis=-1, keepdims=True)
    egrp = ((lane - EXPERT_LANE0) >> 3).astype(F32)
    el = jnp.where(egrp == gidx, lg, neg)
    v1 = jnp.max(el, axis=-1, keepdims=True)
    i1 = jnp.min(jnp.where(el == v1, lanef, big), axis=-1, keepdims=True)
    el2 = jnp.where(lanef == i1, neg, el)
    v2 = jnp.max(el2, axis=-1, keepdims=True)
    i2 = jnp.min(jnp.where(el2 == v2, lanef, big), axis=-1, keepdims=True)
    e2 = jnp.exp(v2 - v1)
    inv = 1.0 / (1.0 + e2)
    w1 = inv * gw
    w2 = (e2 * inv) * gw
    route = jnp.where(lane == 0, i1 - EXPERT_LANE0,
                      jnp.where(lane == 1, i2 - EXPERT_LANE0,
                                jnp.where(lane == 2, w1, jnp.where(lane == 3, w2, 0.0))))
    route_ref[...] = route


def _outproj(x, oa, ob, om, wo, g, wrc, wrh, br):
    T, D = x.shape
    tm = ROW_TILE
    na, nb, nm = oa.shape[1], ob.shape[1], om.shape[1]
    row = lambda i: (i, 0)
    fixed = lambda i: (0, 0)
    return pl.pallas_call(
        functools.partial(_outproj_kernel, na=na, nb=nb),
        grid=(T // tm,),
        in_specs=[pl.BlockSpec((tm, D), row), pl.BlockSpec((tm, na), row),
                  pl.BlockSpec((tm, nb), row), pl.BlockSpec((tm, nm), row),
                  pl.BlockSpec((na + nb + nm, D), fixed), pl.BlockSpec((1, D), fixed),
                  pl.BlockSpec((D, 2 * LANES), fixed), pl.BlockSpec((D, LANES), fixed),
                  pl.BlockSpec((1, LANES), fixed)],
        out_specs=[pl.BlockSpec((tm, D), row), pl.BlockSpec((tm, D), row),
                   pl.BlockSpec((tm, LANES), row)],
        out_shape=[jax.ShapeDtypeStruct((T, D), F32), jax.ShapeDtypeStruct((T, D), BF16),
                   jax.ShapeDtypeStruct((T, LANES), F32)],
        compiler_params=_cparams(("arbitrary",)),
        name="outproj_router",
    )(x, oa, ob, om, wo, g, wrc, wrh, br)


def _expert_kernel(be_ref, nv_ref, x_ref, wg_ref, wu_ref, wd_ref, o_ref):
    i = pl.program_id(0)

    @pl.when(i < nv_ref[0])
    def _():
        x = x_ref[...]
        g = jnp.dot(x, wg_ref[0], preferred_element_type=F32)
        u = jnp.dot(x, wu_ref[0], preferred_element_type=F32)
        a = (g * (1.0 / (1.0 + jnp.exp(-g))) * u).astype(BF16)
        o_ref[...] = jnp.dot(a, wd_ref[0], preferred_element_type=F32)

    @pl.when(i >= nv_ref[0])
    def _():
        o_ref[...] = jnp.zeros_like(o_ref)


def _experts(blk_e, nvalid, xb, wg, wu, wd):
    cap, D = xb.shape
    De = wg.shape[2]
    bm = MOE_TILE
    grid_spec = pltpu.PrefetchScalarGridSpec(
        num_scalar_prefetch=2,
        grid=(cap // bm,),
        in_specs=[pl.BlockSpec((bm, D), lambda i, be, nv: (i, 0)),
                  pl.BlockSpec((1, D, De), lambda i, be, nv: (be[i], 0, 0)),
                  pl.BlockSpec((1, D, De), lambda i, be, nv: (be[i], 0, 0)),
                  pl.BlockSpec((1, De, D), lambda i, be, nv: (be[i], 0, 0))],
        out_specs=pl.BlockSpec((bm, D), lambda i, be, nv: (i, 0)),
    )
    return pl.pallas_call(
        _expert_kernel,
        grid_spec=grid_spec,
        out_shape=jax.ShapeDtypeStruct((cap, D), F32),
        compiler_params=_cparams(("arbitrary",)),
        name="experts",
    )(blk_e, nvalid, xb, wg, wu, wd)


def _final_kernel(x1_ref, y0_ref, y1_ref, r_ref, g_ref, o_ref):
    r = r_ref[...]
    x = x1_ref[...] + (y0_ref[...] * r[:, 2:3] + y1_ref[...] * r[:, 3:4])
    o_ref[...] = _rms(x, g_ref[...])


def _final(x1, y0, y1, route, g):
    T, D = x1.shape
    tm = ROW_TILE
    row = lambda i: (i, 0)
    return pl.pallas_call(
        _final_kernel,
        grid=(T // tm,),
        in_specs=[pl.BlockSpec((tm, D), row), pl.BlockSpec((tm, D), row), pl.BlockSpec((tm, D), row),
                  pl.BlockSpec((tm, LANES), row), pl.BlockSpec((1, D), lambda i: (0, 0))],
        out_specs=pl.BlockSpec((tm, D), row),
        out_shape=jax.ShapeDtypeStruct((T, D), F32),
        compiler_params=_cparams(("arbitrary",)),
        name="combine_final_norm",
    )(x1, y0, y1, route, g)


def _rope_cos_sin(pos, dims, theta):
    inv = jnp.exp(-np.log(theta) * jnp.arange(0, dims, 2, dtype=F32) / dims)
    ang = pos.astype(F32)[:, None] * inv[None, :]
    return jnp.cos(ang), jnp.sin(ang)


def _rope_tables(S):
    rows = S // GRID_W
    row = jnp.repeat(jnp.arange(rows), GRID_W)
    col = jnp.tile(jnp.arange(GRID_W), rows)
    half = HEAD_DIM // 2
    cr, sr = _rope_cos_sin(row, half, AXIAL_THETA)
    cc, sc = _rope_cos_sin(col, half, AXIAL_THETA)
    ca = jnp.concatenate([cr, cr, cc, cc], axis=-1)
    sa = jnp.concatenate([-sr, sr, -sc, sc], axis=-1)
    c, s = _rope_cos_sin(jnp.arange(S), ROPE_DIMS, ROPE_THETA)
    rest = HEAD_DIM - ROPE_DIMS
    cd = jnp.concatenate([c, c, jnp.ones((S, rest), F32)], axis=-1)
    sd = jnp.concatenate([-s, s, jnp.zeros((S, rest), F32)], axis=-1)
    return ca, sa, cd, sd


def _dilated_bias(S):
    d = jnp.arange(S)[:, None] - jnp.arange(S)[None, :]
    ad = jnp.abs(d)
    cnt = jnp.zeros((S, S), F32)
    for window, dil in DILATED_PATTERNS:
        half = window // (2 * dil)
        cnt = cnt + ((d % dil == 0) & (ad <= half * dil)).astype(F32)
    return jnp.where(cnt > 0, jnp.log(jnp.maximum(cnt, 1.0)), -jnp.inf)


def kernel(x_prompt, x_sample, mem_prompt, mem_sample, attn_norm_g, w_in, q_norm_g, k_norm_g,
           mem_norm_g, w_mem_kv, w_out, ffn_norm_g, w_router_group, b_router_group,
           w_router_expert, b_router_expert, w_gate, w_up, w_down, final_norm_g):
    Bp, S, D = x_prompt.shape
    Bs = x_sample.shape[0]
    assert x_sample.shape[1] == S and attn_norm_g.shape[0] == 1
    B = Bp + Bs
    T = B * S
    n_mem = mem_prompt.shape[1]
    n_mix = D // HEAD_DIM
    n_aq, n_akv, n_b, n_m = n_mix // 2, n_mix // 8, n_mix // 4, n_mix // 4
    a_q, a_kv, b_w, m_w = n_aq * HEAD_DIM, n_akv * HEAD_DIM, n_b * HEAD_DIM, n_m * HEAD_DIM
    rep = n_aq // n_akv
    nq = S // Q_TILE

    x = jnp.concatenate([x_prompt, x_sample], axis=0).reshape(T, D)
    mem = jnp.concatenate([mem_prompt, mem_sample], axis=0).reshape(B * n_mem, D)

    tabs = _rope_tables(S)
    proj = _inproj(x, attn_norm_g[0][None], w_in[0].astype(BF16), q_norm_g[0][None],
                   k_norm_g[0][None], tabs, S, (n_aq, n_akv, n_b, n_m))

    c_ak = n_aq
    c_av = c_ak + n_akv
    c_bq = c_av + n_akv
    c_bk = c_bq + n_b
    c_bv = c_bk + n_b
    c_mq = c_bv + n_b

    oa = _attention(
        proj, proj, proj, None, grid=(B, n_akv, nq),
        q_map=lambda b, g, i: (b * nq + i, g),
        k_map=lambda b, g, i: (b, c_ak + g), v_map=lambda b, g, i: (b, c_av + g),
        b_map=None, o_map=lambda b, g, i: (b * nq + i, g),
        rep=rep, sk=S, out_cols=a_q, name="axial_attn")

    ob = _attention(
        proj, proj, proj, _dilated_bias(S), grid=(nq, B, n_b),
        q_map=lambda i, b, h: (b * nq + i, c_bq + h),
        k_map=lambda i, b, h: (b, c_bk + h), v_map=lambda i, b, h: (b, c_bv + h),
        b_map=lambda i, b, h: (i, 0), o_map=lambda i, b, h: (b * nq + i, h),
        rep=1, sk=S, out_cols=b_w, name="dilated_attn")

    memkv = _norm_matmul(mem, mem_norm_g[0][None], w_mem_kv[0].astype(BF16))
    om = _attention(
        proj, memkv, memkv, None, grid=(B, n_m, nq),
        q_map=lambda b, h, i: (b * nq + i, c_mq + h),
        k_map=lambda b, h, i: (b, h), v_map=lambda b, h, i: (b, n_m + h),
        b_map=None, o_map=lambda b, h, i: (b * nq + i, h),
        rep=1, sk=n_mem, out_cols=m_w, name="memory_attn")

    wr = jnp.zeros((D, LANES), F32)
    wr = wr.at[:, :N_GROUPS].set(w_router_group[0])
    wr = wr.at[:, EXPERT_LANE0:EXPERT_LANE0 + N_EXPERTS].set(w_router_expert[0])
    br = jnp.zeros((1, LANES), F32)
    br = br.at[0, :N_GROUPS].set(b_router_group[0].astype(F32))
    br = br.at[0, EXPERT_LANE0:EXPERT_LANE0 + N_EXPERTS].set(b_router_expert[0].astype(F32))
    wr_hi = wr.astype(BF16)
    wr_lo = (wr - wr_hi.astype(F32)).astype(BF16)
    wr_cat = jnp.concatenate([wr_hi, wr_lo], axis=1)

    x1, h2, route = _outproj(x, oa, ob, om, w_out[0].astype(BF16), ffn_norm_g[0][None],
                             wr_cat, wr_hi, br)

    A = T * TOP_K
    bm = MOE_TILE
    flat_e = route[:, :TOP_K].astype(jnp.int32).reshape(A)
    onehot = (flat_e[:, None] == jnp.arange(N_EXPERTS, dtype=jnp.int32)[None, :]).astype(jnp.int32)
    csum = jnp.cumsum(onehot, axis=0)
    rank = jnp.sum((csum - onehot) * onehot, axis=-1)
    counts = csum[-1]
    pcounts = (counts + bm - 1) // bm * bm
    pend = jnp.cumsum(pcounts)
    pstart = pend - pcounts
    dest = pstart[flat_e] + rank
    nblk = A // bm + N_EXPERTS
    cap = nblk * bm
    buf_tok = jnp.zeros((cap,), jnp.int32).at[dest].set(jnp.arange(A, dtype=jnp.int32) // TOP_K)
    blk_e = jnp.minimum(
        jnp.searchsorted(pend, jnp.arange(nblk, dtype=jnp.int32) * bm, side="right"),
        N_EXPERTS - 1).astype(jnp.int32)
    nvalid = (pend[-1:] // bm).astype(jnp.int32)

    xb = h2[buf_tok]
    yb = _experts(blk_e, nvalid, xb, w_gate[0].astype(BF16), w_up[0].astype(BF16),
                  w_down[0].astype(BF16))
    pos = dest.reshape(T, TOP_K)
    y = _final(x1, yb[pos[:, 0]], yb[pos[:, 1]], route, final_norm_g[None])
    y = y.reshape(B, S, D)
    return y[:Bp], y[Bp:]
```
